```python
import math
import jax, jax.numpy as jnp
from jax import lax
import numpy as np

D_MODEL = 2048
BATCH = 8
SEQ = 2048
DEPTH = 1
DEC_BATCH = 1
DEC_SEQ = 16384
PAST_LEN = 128

HEAD_DIM = 128
A_HEADS = 8
A_KV_HEADS = 2
A_GROUP = A_HEADS // A_KV_HEADS
WINDOW = 128
BLOCK = 128
N_BUCKETS = 32
MAX_DISTANCE = 128
B_HEADS = 8
KV_RANK = 512
QK_NOPE = 128
QK_ROPE = 64
V_DIM = 128
ROPE_THETA = 10000.0
D_FF = 4 * D_MODEL
EPS = 1e-6
NEG = -1e30

A_Q = A_HEADS * HEAD_DIM
A_KV = A_KV_HEADS * HEAD_DIM
B_QN = B_HEADS * QK_NOPE
B_QR = B_HEADS * QK_ROPE
D_IN = A_Q + 2 * A_KV + B_QN + B_QR + KV_RANK + QK_ROPE
D_MIX_OUT = A_HEADS * HEAD_DIM + B_HEADS * V_DIM

kernel_name = "hymba_swa_mla_adaln_encoder"


def rmsnorm(x, g):
    xf = x.astype(jnp.float32)
    y = xf * lax.rsqrt(jnp.mean(xf * xf, axis=-1, keepdims=True) + EPS)
    return (y * g.astype(jnp.float32)).astype(x.dtype)


def t5_bucket(rel):
    half = N_BUCKETS // 2
    max_exact = half // 2
    ret = jnp.where(rel > 0, half, 0)
    n = jnp.abs(rel)
    nf = jnp.maximum(n, 1).astype(jnp.float32)
    large = max_exact + (jnp.log(nf / max_exact) / math.log(MAX_DISTANCE / max_exact)
                         * (half - max_exact)).astype(jnp.int32)
    large = jnp.minimum(large, half - 1)
    return ret + jnp.where(n < max_exact, n, large)


def rope(x, pos):
    half = QK_ROPE // 2
    inv = ROPE_THETA ** (-jnp.arange(half, dtype=jnp.float32) / half)
    ang = pos.astype(jnp.float32)[:, None] * inv[None, :]
    cos = jnp.cos(ang)[:, None, :]
    sin = jnp.sin(ang)[:, None, :]
    xf = x.astype(jnp.float32)
    x1, x2 = xf[..., :half], xf[..., half:]
    return jnp.concatenate([x1 * cos - x2 * sin, x1 * sin + x2 * cos], axis=-1).astype(x.dtype)


def window_attention(q, k, v, sink, rel_bias):
    B, S = q.shape[0], q.shape[1]
    nb = S // BLOCK
    qb = q.reshape(B, nb, BLOCK, A_KV_HEADS, A_GROUP, HEAD_DIM)
    pad = ((0, 0), (BLOCK, BLOCK), (0, 0), (0, 0))
    kp = jnp.pad(k, pad).reshape(B, nb + 2, BLOCK, A_KV_HEADS, HEAD_DIM)
    vp = jnp.pad(v, pad).reshape(B, nb + 2, BLOCK, A_KV_HEADS, HEAD_DIM)
    kb = jnp.concatenate([kp[:, :-2], kp[:, 1:-1], kp[:, 2:]], axis=2)
    vb = jnp.concatenate([vp[:, :-2], vp[:, 1:-1], vp[:, 2:]], axis=2)
    s = jnp.einsum('bnqgrd,bnkgd->bngrqk', qb, kb,
                   preferred_element_type=jnp.float32) * (HEAD_DIM ** -0.5)
    q_off = jnp.arange(BLOCK)[:, None]
    k_off = jnp.arange(3 * BLOCK)[None, :] - BLOCK
    rel = k_off - q_off
    bias = rel_bias.astype(jnp.float32)[t5_bucket(rel)]
    bias = bias.transpose(2, 0, 1).reshape(A_KV_HEADS, A_GROUP, BLOCK, 3 * BLOCK)
    k_abs = jnp.arange(nb)[:, None] * BLOCK + k_off
    valid = (jnp.abs(rel) <= WINDOW)[None] & ((k_abs >= 0) & (k_abs < S))[:, None, :]
    s = jnp.where(valid[None, :, None, None], s + bias, NEG)
    sink_l = sink.astype(jnp.float32).reshape(A_KV_HEADS, A_GROUP)[None, None, :, :, None, None]
    m = jnp.maximum(jnp.max(s, axis=-1, keepdims=True), sink_l)
    p = jnp.exp(s - m)
    denom = jnp.sum(p, axis=-1, keepdims=True) + jnp.exp(sink_l - m)
    p = (p / denom).astype(v.dtype)
    o = jnp.einsum('bngrqk,bnkgd->bnqgrd', p, vb)
    return o.reshape(B, S, A_HEADS * HEAD_DIM)


def latent_attention(q_nope, q_rope, c_kv, k_rope, g_kv, w_kv_b):
    B, S = q_nope.shape[0], q_nope.shape[1]
    pos = jnp.arange(S)
    qr = rope(q_rope.reshape(B, S, B_HEADS, QK_ROPE), pos)
    kr = rope(k_rope[:, :, None, :], pos)[:, :, 0]
    kv = (rmsnorm(c_kv, g_kv) @ w_kv_b).reshape(B, S, B_HEADS, QK_NOPE + V_DIM)
    k_nope, v = kv[..., :QK_NOPE], kv[..., QK_NOPE:]
    qn = q_nope.reshape(B, S, B_HEADS, QK_NOPE)
    scale = (QK_NOPE + QK_ROPE) ** -0.5
    nb = S // BLOCK
    qn_b = qn.reshape(B, nb, BLOCK, B_HEADS, QK_NOPE).transpose(1, 0, 2, 3, 4)
    qr_b = qr.reshape(B, nb, BLOCK, B_HEADS, QK_ROPE).transpose(1, 0, 2, 3, 4)

    def attend(blk):
        qn_i, qr_i = blk
        s = (jnp.einsum('bqhd,bkhd->bhqk', qn_i, k_nope, preferred_element_type=jnp.float32)
             + jnp.einsum('bqhr,bkr->bhqk', qr_i, kr, preferred_element_type=jnp.float32)) * scale
        p = jax.nn.softmax(s, axis=-1).astype(v.dtype)
        return jnp.einsum('bhqk,bkhd->bqhd', p, v)

    o = lax.map(attend, (qn_b, qr_b))
    return o.transpose(1, 0, 2, 3, 4).reshape(B, S, B_HEADS * V_DIM)


def trunk(x, c, w_ada, b_ada, g_mix, w_in, sink, g_kv, w_kv_b, w_o,
          g_mlp, w_ff1, w_ff2, rel_bias, g_final):
    B, S, _ = x.shape
    splits = np.cumsum([A_Q, A_KV, A_KV, B_QN, B_QR, KV_RANK]).tolist()
    for l in range(DEPTH):
        mod = jax.nn.silu(c) @ w_ada[l] + b_ada[l]
        sh1, sc1, gt1, sh2, sc2, gt2 = jnp.split(mod[:, None, :], 6, axis=-1)
        h = rmsnorm(x, g_mix[l]) * (1 + sc1) + sh1
        proj = h @ w_in[l]
        qa, ka, va, qn, qr, ckv, kr = jnp.split(proj, splits, axis=-1)
        out_a = window_attention(qa.reshape(B, S, A_HEADS, HEAD_DIM),
                                 ka.reshape(B, S, A_KV_HEADS, HEAD_DIM),
                                 va.reshape(B, S, A_KV_HEADS, HEAD_DIM),
                                 sink[l], rel_bias)
        out_b = latent_attention(qn, qr, ckv, kr, g_kv[l], w_kv_b[l])
        x = x + gt1 * (jnp.concatenate([out_a, out_b], axis=-1) @ w_o[l])
        h = rmsnorm(x, g_mlp[l]) * (1 + sc2) + sh2
        f = jnp.square(jax.nn.relu(h @ w_ff1[l])) @ w_ff2[l]
        x = x + gt2 * f
    return rmsnorm(x, g_final)


def setup_inputs(seed: int = 0) -> dict:
    key = jax.random.key(seed)
    ks = jax.random.split(key, 20)
    f32 = jnp.float32

    def nrm(k, shape, scale):
        return jax.random.normal(k, shape, f32) * scale

    return {
        "x_prompt": nrm(ks[0], (BATCH, SEQ, D_MODEL), 1.0),
        "x_sample": nrm(ks[1], (DEC_BATCH, DEC_SEQ, D_MODEL), 1.0),
        "c_prompt": nrm(ks[2], (BATCH, D_MODEL), 1.0),
        "c_sample": nrm(ks[3], (DEC_BATCH, D_MODEL), 1.0),
        "w_ada": nrm(ks[4], (DEPTH, D_MODEL, 6 * D_MODEL), 0.5 * D_MODEL ** -0.5),
        "b_ada": nrm(ks[5], (DEPTH, 6 * D_MODEL), 0.02),
        "g_mix": 1.0 + nrm(ks[6], (DEPTH, D_MODEL), 0.02),
        "w_in": nrm(ks[7], (DEPTH, D_MODEL, D_IN), D_MODEL ** -0.5),
        "sink": nrm(ks[8], (DEPTH, A_HEADS), 0.5),
        "g_kv": 1.0 + nrm(ks[9], (DEPTH, KV_RANK), 0.02),
        "w_kv_b": nrm(ks[10], (DEPTH, KV_RANK, B_HEADS * (QK_NOPE + V_DIM)), KV_RANK ** -0.5),
        "w_o": nrm(ks[11], (DEPTH, D_MIX_OUT, D_MODEL), D_MIX_OUT ** -0.5),
        "g_mlp": 1.0 + nrm(ks[12], (DEPTH, D_MODEL), 0.02),
        "w_ff1": nrm(ks[13], (DEPTH, D_MODEL, D_FF), D_MODEL ** -0.5),
        "w_ff2": nrm(ks[14], (DEPTH, D_FF, D_MODEL), D_FF ** -0.5),
        "rel_bias": nrm(ks[15], (N_BUCKETS, A_HEADS), 0.5),
        "g_final": 1.0 + nrm(ks[16], (D_MODEL,), 0.02),
    }


def reference(x_prompt, x_sample, c_prompt, c_sample, w_ada, b_ada, g_mix, w_in, sink,
              g_kv, w_kv_b, w_o, g_mlp, w_ff1, w_ff2, rel_bias, g_final):
    y_prompt = trunk(x_prompt, c_prompt, w_ada, b_ada, g_mix, w_in, sink, g_kv, w_kv_b, w_o,
                     g_mlp, w_ff1, w_ff2, rel_bias, g_final)
    y_sample = trunk(x_sample, c_sample, w_ada, b_ada, g_mix, w_in, sink, g_kv, w_kv_b, w_o,
                     g_mlp, w_ff1, w_ff2, rel_bias, g_final)
    return (y_prompt, y_sample)
```

```python
import functools
import math

import jax
import jax.numpy as jnp
import numpy as np
from jax import lax
from jax.experimental import pallas as pl
from jax.experimental.pallas import tpu as pltpu

D_MODEL = 2048
DEPTH = 1
HEAD_DIM = 128
A_HEADS = 8
A_KV_HEADS = 2
A_GROUP = A_HEADS // A_KV_HEADS
WINDOW = 128
BLOCK = 128
N_BUCKETS = 32
MAX_DISTANCE = 128
B_HEADS = 8
KV_RANK = 512
QK_NOPE = 128
QK_ROPE = 64
V_DIM = 128
ROPE_THETA = 10000.0
D_FF = 4 * D_MODEL
EPS = 1e-6
NEG = -1e30

A_Q = A_HEADS * HEAD_DIM
A_KV = A_KV_HEADS * HEAD_DIM
B_QN = B_HEADS * QK_NOPE
B_QR = B_HEADS * QK_ROPE

LANES = 128
MLA_QK = 2 * LANES
VMEM_LIMIT = 56 * 1024 * 1024

C_QA = 0
C_KA = C_QA + A_Q
C_VA = C_KA + A_KV
C_QM = C_VA + A_KV
C_CKV = C_QM + B_HEADS * MLA_QK
C_KR = C_CKV + KV_RANK
C_END = C_KR + LANES

F32 = jnp.float32
BF16 = jnp.bfloat16


def _params(sem):
    return pltpu.CompilerParams(dimension_semantics=sem, vmem_limit_bytes=VMEM_LIMIT)


def _resident(shape):
    return pl.BlockSpec(shape, lambda *_: (0,) * len(shape), pipeline_mode=pl.Buffered(1))


def _rms_scale(x):
    return lax.rsqrt(jnp.mean(x * x, axis=-1, keepdims=True) + EPS)


def _ada_kernel(c_ref, w_ref, b_ref, o_ref):
    c = c_ref[...]
    h = c * jax.nn.sigmoid(c)
    o_ref[...] = jnp.dot(h.astype(BF16), w_ref[...].astype(BF16), preferred_element_type=F32) + b_ref[...]


def _ada_mod(c_all, w, b):
    rows, n = c_all.shape[0], w.shape[1]
    tn = 1024
    return pl.pallas_call(
        _ada_kernel,
        grid=(n // tn,),
        in_specs=[
            pl.BlockSpec((rows, D_MODEL), lambda j: (0, 0)),
            pl.BlockSpec((D_MODEL, tn), lambda j: (0, j)),
            pl.BlockSpec((1, tn), lambda j: (0, j)),
        ],
        out_specs=pl.BlockSpec((rows, tn), lambda j: (0, j)),
        out_shape=jax.ShapeDtypeStruct((rows, n), F32),
        compiler_params=_params(("parallel",)),
        name="ada_mod",
    )(c_all, w, b)


def _inproj_kernel(x_ref, mod_ref, gmix_ref, gkv_ref, w_ref, wkv_ref, ra_ref, rb_ref,
                   qa_ref, ka_ref, va_ref, qm_ref, km_ref, vm_ref):
    x = x_ref[...]
    y = (x * _rms_scale(x)) * gmix_ref[...]
    h = (y * (1.0 + mod_ref[1:2, :]) + mod_ref[0:1, :]).astype(BF16)

    def proj(lo, hi):
        return jnp.dot(h, w_ref[:, lo:hi], preferred_element_type=F32)

    qa_ref[...] = (proj(C_QA, C_KA) * (HEAD_DIM ** -0.5)).astype(BF16)
    ka_ref[...] = proj(C_KA, C_VA).astype(BF16)
    va_ref[...] = proj(C_VA, C_QM).astype(BF16)

    ra = ra_ref[...]
    rb = rb_ref[...]

    def rope(blk):
        return blk * ra + pltpu.roll(blk, QK_ROPE, 1) * rb

    scale_b = (QK_NOPE + QK_ROPE) ** -0.5
    for hd in range(B_HEADS):
        q = proj(C_QM + hd * MLA_QK, C_QM + (hd + 1) * MLA_QK)
        qm_ref[:, hd * MLA_QK:hd * MLA_QK + LANES] = (q[:, :LANES] * scale_b).astype(BF16)
        qm_ref[:, hd * MLA_QK + LANES:(hd + 1) * MLA_QK] = (rope(q[:, LANES:]) * scale_b).astype(BF16)

    ckv = proj(C_CKV, C_KR)
    cn = ((ckv * _rms_scale(ckv)) * gkv_ref[...]).astype(BF16)
    kr = rope(proj(C_KR, C_END)).astype(BF16)
    kn = jnp.dot(cn, wkv_ref[:, :B_HEADS * QK_NOPE], preferred_element_type=F32).astype(BF16)
    for hd in range(B_HEADS):
        km_ref[:, hd * MLA_QK:hd * MLA_QK + LANES] = kn[:, hd * QK_NOPE:(hd + 1) * QK_NOPE]
        km_ref[:, hd * MLA_QK + LANES:(hd + 1) * MLA_QK] = kr
    vm_ref[...] = jnp.dot(cn, wkv_ref[:, B_HEADS * QK_NOPE:], preferred_element_type=F32).astype(BF16)


def _in_proj(x2d, mod3, gmix, gkv, w_all, w_kvb, rope_a, rope_b, *, row_off, seq):
    t = x2d.shape[0]
    tm = 512
    pos_blocks = seq // tm
    tok = lambda width: pl.BlockSpec((tm, width), lambda i: (i, 0))
    return pl.pallas_call(
        _inproj_kernel,
        grid=(t // tm,),
        in_specs=[
            tok(D_MODEL),
            pl.BlockSpec((None, 6, D_MODEL), lambda i: (row_off + (i * tm) // seq, 0, 0)),
            _resident((1, D_MODEL)),
            _resident((1, KV_RANK)),
            _resident((D_MODEL, C_END)),
            _resident((KV_RANK, B_HEADS * (QK_NOPE + V_DIM))),
            pl.BlockSpec((tm, LANES), lambda i: (i % pos_blocks, 0)),
            pl.BlockSpec((tm, LANES), lambda i: (i % pos_blocks, 0)),
        ],
        out_specs=[tok(A_Q), tok(A_KV), tok(A_KV), tok(B_HEADS * MLA_QK), tok(B_HEADS * MLA_QK),
                   tok(B_HEADS * V_DIM)],
        out_shape=[
            jax.ShapeDtypeStruct((t, A_Q), BF16),
            jax.ShapeDtypeStruct((t, A_KV), BF16),
            jax.ShapeDtypeStruct((t, A_KV), BF16),
            jax.ShapeDtypeStruct((t, B_HEADS * MLA_QK), BF16),
            jax.ShapeDtypeStruct((t, B_HEADS * MLA_QK), BF16),
            jax.ShapeDtypeStruct((t, B_HEADS * V_DIM), BF16),
        ],
        compiler_params=_params(("parallel",)),
        name="in_proj",
    )(x2d, mod3, gmix, gkv, w_all, w_kvb, rope_a, rope_b)


def _bias_kernel(bucket_ref, rel_ref, o_ref):
    hd = pl.program_id(0)
    bucket = bucket_ref[...]
    acc = jnp.zeros(bucket.shape, F32)
    for b in range(N_BUCKETS):
        acc = jnp.where(bucket == b, rel_ref[b, hd], acc)
    o_ref[...] = acc


def _bias_table(bucket, rel_bias):
    return pl.pallas_call(
        _bias_kernel,
        grid=(A_HEADS,),
        in_specs=[
            pl.BlockSpec((BLOCK, 3 * BLOCK), lambda h: (0, 0)),
            pl.BlockSpec(memory_space=pltpu.SMEM),
        ],
        out_specs=pl.BlockSpec((None, BLOCK, 3 * BLOCK), lambda h: (h, 0, 0)),
        out_shape=jax.ShapeDtypeStruct((A_HEADS, BLOCK, 3 * BLOCK), F32),
        compiler_params=_params(("parallel",)),
        name="rel_bias_table",
    )(bucket, rel_bias)


def _win_kernel(q_ref, kp_ref, km_ref, kn_ref, vp_ref, vm_ref, vn_ref, bias_ref, sink_ref, o_ref, *, seq, tq):
    nsub = tq // BLOCK
    rows = A_GROUP * BLOCK
    first_block = pl.program_id(2) * nsub
    q_off = lax.broadcasted_iota(jnp.int32, (rows, 3 * BLOCK), 0) & (BLOCK - 1)
    k_off = lax.broadcasted_iota(jnp.int32, (rows, 3 * BLOCK), 1) - BLOCK
    in_window = jnp.abs(k_off - q_off) <= WINDOW
    bias = bias_ref[...]
    sink = sink_ref[:, 0:1]

    def band(prev_ref, main_ref, next_ref, j):
        lo = prev_ref[...] if j == 0 else main_ref[(j - 1) * BLOCK:j * BLOCK, :]
        hi = next_ref[...] if j == nsub - 1 else main_ref[(j + 1) * BLOCK:(j + 2) * BLOCK, :]
        return jnp.concatenate([lo, main_ref[j * BLOCK:(j + 1) * BLOCK, :], hi], axis=0)

    for j in range(nsub):
        qs = jnp.concatenate(
            [q_ref[j * BLOCK:(j + 1) * BLOCK, r * HEAD_DIM:(r + 1) * HEAD_DIM] for r in range(A_GROUP)], axis=0)
        kb = band(kp_ref, km_ref, kn_ref, j)
        vb = band(vp_ref, vm_ref, vn_ref, j)
        s = lax.dot_general(qs, kb, (((1,), (1,)), ((), ())), preferred_element_type=F32)
        k_abs = (first_block + j) * BLOCK + k_off
        valid = in_window & (k_abs >= 0) & (k_abs < seq)
        s = jnp.where(valid, s + bias, NEG)
        m = jnp.maximum(jnp.max(s, axis=-1, keepdims=True), sink)
        p = jnp.exp(s - m)
        denom = jnp.sum(p, axis=-1, keepdims=True) + jnp.exp(sink - m)
        o = jnp.dot(p.astype(BF16), vb, preferred_element_type=F32) / denom
        for r in range(A_GROUP):
            o_ref[j * BLOCK:(j + 1) * BLOCK, r * HEAD_DIM:(r + 1) * HEAD_DIM] = (
                o[r * BLOCK:(r + 1) * BLOCK, :].astype(BF16))


def _win_attn(qa, ka, va, bias2, sink2, *, seq, nbatch):
    t = qa.shape[0]
    tq = 512
    nsub = tq // BLOCK
    qt = seq // tq
    last = t // BLOCK - 1
    main = lambda width: pl.BlockSpec((tq, width), lambda b, g, i: (b * qt + i, g))
    prev = pl.BlockSpec((BLOCK, HEAD_DIM), lambda b, g, i: (jnp.maximum((b * qt + i) * nsub - 1, 0), g))
    nxt = pl.BlockSpec((BLOCK, HEAD_DIM), lambda b, g, i: (jnp.minimum((b * qt + i + 1) * nsub, last), g))
    rows = A_GROUP * BLOCK
    return pl.pallas_call(
        functools.partial(_win_kernel, seq=seq, tq=tq),
        grid=(nbatch, A_KV_HEADS, qt),
        in_specs=[
            main(A_GROUP * HEAD_DIM),
            prev, main(HEAD_DIM), nxt,
            prev, main(HEAD_DIM), nxt,
            pl.BlockSpec((None, rows, 3 * BLOCK), lambda b, g, i: (g, 0, 0)),
            pl.BlockSpec((None, rows, LANES), lambda b, g, i: (g, 0, 0)),
        ],
        out_specs=main(A_GROUP * HEAD_DIM),
        out_shape=jax.ShapeDtypeStruct((t, A_Q), BF16),
        compiler_params=_params(("parallel", "parallel", "parallel")),
        name="win_attn",
    )(qa, ka, ka, ka, va, va, va, bias2, sink2)


def _mla_kernel(q_ref, k_ref, v_ref, o_ref, m_ref, l_ref, acc_ref, *, seq, tk):
    m_ref[...] = jnp.full(m_ref.shape, -jnp.inf, F32)
    l_ref[...] = jnp.zeros(l_ref.shape, F32)
    acc_ref[...] = jnp.zeros(acc_ref.shape, F32)
    q = q_ref[...]

    def chunk(c, carry):
        start = pl.multiple_of(c * tk, tk)
        k = k_ref[pl.ds(start, tk), :]
        v = v_ref[pl.ds(start, tk), :]
        s = lax.dot_general(q, k, (((1,), (1,)), ((), ())), preferred_element_type=F32)
        m_prev = m_ref[...]
        m_new = jnp.maximum(m_prev, jnp.max(s, axis=-1, keepdims=True))
        p = jnp.exp(s - m_new[:, 0:1])
        alpha = jnp.exp(m_prev - m_new)
        l_ref[...] = alpha * l_ref[...] + jnp.sum(p, axis=-1, keepdims=True)
        acc_ref[...] = alpha * acc_ref[...] + jnp.dot(p.astype(BF16), v, preferred_element_type=F32)
        m_ref[...] = m_new
        return carry

    lax.fori_loop(0, seq // tk, chunk, 0)
    o_ref[...] = (acc_ref[...] / l_ref[...]).astype(BF16)


def _mla_attn(qm, km, vm, *, seq, nbatch):
    t = qm.shape[0]
    tq = 1024
    tk = 512
    qt = seq // tq
    return pl.pallas_call(
        functools.partial(_mla_kernel, seq=seq, tk=tk),
        grid=(nbatch, B_HEADS, qt),
        in_specs=[
            pl.BlockSpec((tq, MLA_QK), lambda b, h, i: (b * qt + i, h)),
            pl.BlockSpec((seq, MLA_QK), lambda b, h, i: (b, h)),
            pl.BlockSpec((seq, V_DIM), lambda b, h, i: (b, h)),
        ],
        out_specs=pl.BlockSpec((tq, V_DIM), lambda b, h, i: (b * qt + i, h)),
        out_shape=jax.ShapeDtypeStruct((t, B_HEADS * V_DIM), BF16),
        scratch_shapes=[pltpu.VMEM((tq, LANES), F32)] * 3,
        compiler_params=_params(("parallel", "parallel", "parallel")),
        name="mla_attn",
    )(qm, km, vm)


def _outproj_kernel(x_ref, mod_ref, oa_ref, ob_ref, wo_ref, gmlp_ref, x1_ref, h2_ref):
    mix = (jnp.dot(oa_ref[...], wo_ref[:A_Q, :], preferred_element_type=F32)
           + jnp.dot(ob_ref[...], wo_ref[A_Q:, :], preferred_element_type=F32))
    x1 = x_ref[...] + mod_ref[2:3, :] * mix
    x1_ref[...] = x1
    y = (x1 * _rms_scale(x1)) * gmlp_ref[...]
    h2_ref[...] = (y * (1.0 + mod_ref[4:5, :]) + mod_ref[3:4, :]).astype(BF16)


def _out_proj(x2d, mod3, out_a, out_b, w_o, gmlp, *, row_off, seq):
    t = x2d.shape[0]
    tm = 512
    tok = lambda width: pl.BlockSpec((tm, width), lambda i: (i, 0))
    return pl.pallas_call(
        _outproj_kernel,
        grid=(t // tm,),
        in_specs=[
            tok(D_MODEL),
            pl.BlockSpec((None, 6, D_MODEL), lambda i: (row_off + (i * tm) // seq, 0, 0)),
            tok(A_Q),
            tok(B_HEADS * V_DIM),
            _resident((A_Q + B_HEADS * V_DIM, D_MODEL)),
            _resident((1, D_MODEL)),
        ],
        out_specs=[tok(D_MODEL), tok(D_MODEL)],
        out_shape=[jax.ShapeDtypeStruct((t, D_MODEL), F32), jax.ShapeDtypeStruct((t, D_MODEL), BF16)],
        compiler_params=_params(("parallel",)),
        name="out_proj",
    )(x2d, mod3, out_a, out_b, w_o, gmlp)


def _mlp_kernel(h2_ref, x1_ref, mod_ref, w1_ref, w2_ref, gfin_ref, o_ref, *, final_norm):
    j = pl.program_id(1)
    u = jnp.maximum(jnp.dot(h2_ref[...], w1_ref[...], preferred_element_type=F32), 0.0)
    f = jnp.dot((u * u).astype(BF16), w2_ref[...], preferred_element_type=F32)

    @pl.when(j == 0)
    def _():
        o_ref[...] = f

    @pl.when(j > 0)
    def _():
        o_ref[...] += f

    @pl.when(j == pl.num_programs(1) - 1)
    def _():
        x2 = x1_ref[...] + mod_ref[5:6, :] * o_ref[...]
        if final_norm:
            x2 = (x2 * _rms_scale(x2)) * gfin_ref[...]
        o_ref[...] = x2


def _mlp(h2, x1, mod3, w1, w2, gfin, *, row_off, seq, final_norm):
    t = h2.shape[0]
    tm = 512
    fc = 512
    tok = lambda width: pl.BlockSpec((tm, width), lambda i, j: (i, 0))
    return pl.pallas_call(
        functools.partial(_mlp_kernel, final_norm=final_norm),
        grid=(t // tm, D_FF // fc),
        in_specs=[
            tok(D_MODEL),
            tok(D_MODEL),
            pl.BlockSpec((None, 6, D_MODEL), lambda i, j: (row_off + (i * tm) // seq, 0, 0)),
            pl.BlockSpec((D_MODEL, fc), lambda i, j: (0, j)),
            pl.BlockSpec((fc, D_MODEL), lambda i, j: (j, 0)),
            pl.BlockSpec((1, D_MODEL), lambda i, j: (0, 0)),
        ],
        out_specs=tok(D_MODEL),
        out_shape=jax.ShapeDtypeStruct((t, D_MODEL), F32),
        compiler_params=_params(("parallel", "arbitrary")),
        name="mlp",
    )(h2, x1, mod3, w1, w2, gfin)


def _t5_bucket(rel):
    half = N_BUCKETS // 2
    max_exact = half // 2
    ret = jnp.where(rel > 0, half, 0)
    n = jnp.abs(rel)
    nf = jnp.maximum(n, 1).astype(jnp.float32)
    large = max_exact + (jnp.log(nf / max_exact) / math.log(MAX_DISTANCE / max_exact)
                         * (half - max_exact)).astype(jnp.int32)
    large = jnp.minimum(large, half - 1)
    return ret + jnp.where(n < max_exact, n, large)


def _bucket_table():
    q_off = jnp.arange(BLOCK)[:, None]
    k_off = jnp.arange(3 * BLOCK)[None, :] - BLOCK
    return _t5_bucket(k_off - q_off).astype(jnp.int32)


def _rope_tables(seq):
    half = QK_ROPE // 2
    inv = ROPE_THETA ** (-jnp.arange(half, dtype=jnp.float32) / half)
    ang = jnp.arange(seq).astype(jnp.float32)[:, None] * inv[None, :]
    cos, sin = jnp.cos(ang), jnp.sin(ang)
    pad = jnp.zeros((seq, LANES - QK_ROPE), F32)
    return (jnp.concatenate([cos, cos, pad], axis=1), jnp.concatenate([-sin, sin, pad], axis=1))


def _swap_halves(w):
    half = QK_ROPE // 2
    return jnp.concatenate([w[..., half:], w[..., :half]], axis=-1)


def _arrange_w_in(w):
    d = w.shape[0]
    o = np.cumsum([0, A_Q, A_KV, A_KV, B_QN, B_QR, KV_RANK, QK_ROPE]).tolist()
    wqn = w[:, o[3]:o[4]].reshape(d, B_HEADS, QK_NOPE)
    wqr = w[:, o[4]:o[5]].reshape(d, B_HEADS, QK_ROPE)
    wqm = jnp.concatenate([wqn, wqr, _swap_halves(wqr)], axis=-1).reshape(d, B_HEADS * MLA_QK)
    wkr = w[:, o[6]:o[7]]
    return jnp.concatenate([w[:, :o[3]], wqm, w[:, o[5]:o[6]], wkr, _swap_halves(wkr)], axis=1).astype(BF16)


def _arrange_w_kvb(w):
    w3 = w.reshape(KV_RANK, B_HEADS, QK_NOPE + V_DIM)
    return jnp.concatenate([w3[..., :QK_NOPE].reshape(KV_RANK, -1), w3[..., QK_NOPE:].reshape(KV_RANK, -1)],
                           axis=1).astype(BF16)


def _trunk(x, mod3, row_off, layers, bias2, g_final):
    nbatch, seq, _ = x.shape
    x2d = x.reshape(nbatch * seq, D_MODEL)
    rope_a, rope_b = _rope_tables(seq)
    for l, p in enumerate(layers):
        kw = dict(row_off=row_off, seq=seq)
        qa, ka, va, qm, km, vm = _in_proj(x2d, mod3[l], p["g_mix"], p["g_kv"], p["w_in"], p["w_kv_b"],
                                          rope_a, rope_b, **kw)
        out_a = _win_attn(qa, ka, va, bias2, p["sink"], seq=seq, nbatch=nbatch)
        out_b = _mla_attn(qm, km, vm, seq=seq, nbatch=nbatch)
        x1, h2 = _out_proj(x2d, mod3[l], out_a, out_b, p["w_o"], p["g_mlp"], **kw)
        x2d = _mlp(h2, x1, mod3[l], p["w_ff1"], p["w_ff2"], g_final, final_norm=(l == len(layers) - 1), **kw)
    return x2d.reshape(nbatch, seq, D_MODEL)


def kernel(x_prompt, x_sample, c_prompt, c_sample, w_ada, b_ada, g_mix, w_in, sink, g_kv, w_kv_b, w_o, g_mlp,
           w_ff1, w_ff2, rel_bias, g_final):
    n_prompt, n_sample = c_prompt.shape[0], c_sample.shape[0]
    rows = -(-(n_prompt + n_sample) // 16) * 16
    c_all = jnp.concatenate(
        [c_prompt, c_sample, jnp.zeros((rows - n_prompt - n_sample, D_MODEL), F32)], axis=0)

    depth = w_in.shape[0]
    mod3, layers = [], []
    for l in range(depth):
        mod = _ada_mod(c_all, w_ada[l], b_ada[l].reshape(1, -1))
        mod3.append(mod.reshape(rows, 6, D_MODEL))
        sink_rows = jnp.repeat(sink[l].astype(F32), BLOCK).reshape(A_KV_HEADS, A_GROUP * BLOCK, 1)
        layers.append(dict(
            g_mix=g_mix[l].reshape(1, -1), g_kv=g_kv[l].reshape(1, -1), g_mlp=g_mlp[l].reshape(1, -1),
            w_in=_arrange_w_in(w_in[l]), w_kv_b=_arrange_w_kvb(w_kv_b[l]), w_o=w_o[l].astype(BF16),
            w_ff1=w_ff1[l].astype(BF16), w_ff2=w_ff2[l].astype(BF16),
            sink=jnp.broadcast_to(sink_rows, (A_KV_HEADS, A_GROUP * BLOCK, LANES)),
        ))
    bias2 = _bias_table(_bucket_table(), rel_bias.astype(F32)).reshape(A_KV_HEADS, A_GROUP * BLOCK, 3 * BLOCK)
    gfin = g_final.reshape(1, -1)

    y_prompt = _trunk(x_prompt, mod3, 0, layers, bias2, gfin)
    y_sample = _trunk(x_sample, mod3, n_prompt, layers, bias2, gfin)
    return (y_prompt, y_sample)
```

```python
import functools
import math

import jax
import jax.numpy as jnp
import numpy as np
from jax import lax
from jax.experimental import pallas as pl
from jax.experimental.pallas import tpu as pltpu

D_MODEL = 2048
DEPTH = 1
HEAD_DIM = 128
A_HEADS = 8
A_KV_HEADS = 2
A_GROUP = A_HEADS // A_KV_HEADS
WINDOW = 128
BLOCK = 128
N_BUCKETS = 32
MAX_DISTANCE = 128
B_HEADS = 8
KV_RANK = 512
QK_NOPE = 128
QK_ROPE = 64
V_DIM = 128
ROPE_THETA = 10000.0
D_FF = 4 * D_MODEL
EPS = 1e-6
NEG = -1e30

A_Q = A_HEADS * HEAD_DIM
A_KV = A_KV_HEADS * HEAD_DIM
B_QN = B_HEADS * QK_NOPE
B_QR = B_HEADS * QK_ROPE

LANES = 128
MLA_QK = 2 * LANES
VMEM_LIMIT = 56 * 1024 * 1024

C_QA = 0
C_KA = C_QA + A_Q
C_VA = C_KA + A_KV
C_QM = C_VA + A_KV
C_CKV = C_QM + B_HEADS * MLA_QK
C_KR = C_CKV + KV_RANK
C_END = C_KR + LANES

F32 = jnp.float32
BF16 = jnp.bfloat16


def _params(sem):
    return pltpu.CompilerParams(dimension_semantics=sem, vmem_limit_bytes=VMEM_LIMIT)


def _resident(shape):
    return pl.BlockSpec(shape, lambda *_: (0,) * len(shape), pipeline_mode=pl.Buffered(1))


def _rms_scale(x):
    return lax.rsqrt(jnp.mean(x * x, axis=-1, keepdims=True) + EPS)


def _ada_kernel(c_ref, w_ref, b_ref, o_ref):
    c = c_ref[...]
    h = c * jax.nn.sigmoid(c)
    o_ref[...] = jnp.dot(h.astype(BF16), w_ref[...].astype(BF16), preferred_element_type=F32) + b_ref[...]


def _ada_mod(c_all, w, b):
    rows, n = c_all.shape[0], w.shape[1]
    tn = 1024
    return pl.pallas_call(
        _ada_kernel,
        grid=(n // tn,),
        in_specs=[
            pl.BlockSpec((rows, D_MODEL), lambda j: (0, 0)),
            pl.BlockSpec((D_MODEL, tn), lambda j: (0, j)),
            pl.BlockSpec((1, tn), lambda j: (0, j)),
        ],
        out_specs=pl.BlockSpec((rows, tn), lambda j: (0, j)),
        out_shape=jax.ShapeDtypeStruct((rows, n), F32),
        compiler_params=_params(("parallel",)),
        name="ada_mod",
    )(c_all, w, b)


def _inproj_kernel(x_ref, mod_ref, gmix_ref, gkv_ref, w_ref, wkv_ref, ra_ref, rb_ref,
                   qa_ref, ka_ref, va_ref, qm_ref, km_ref, vm_ref):
    x = x_ref[...]
    y = (x * _rms_scale(x)) * gmix_ref[...]
    h = (y * (1.0 + mod_ref[1:2, :]) + mod_ref[0:1, :]).astype(BF16)

    def proj(lo, hi):
        return jnp.dot(h, w_ref[:, lo:hi], preferred_element_type=F32)

    qa_ref[...] = (proj(C_QA, C_KA) * (HEAD_DIM ** -0.5)).astype(BF16)
    ka_ref[...] = proj(C_KA, C_VA).astype(BF16)
    va_ref[...] = proj(C_VA, C_QM).astype(BF16)

    ra = ra_ref[...]
    rb = rb_ref[...]

    def rope(blk):
        return blk * ra + pltpu.roll(blk, QK_ROPE, 1) * rb

    scale_b = (QK_NOPE + QK_ROPE) ** -0.5 * math.log2(math.e)
    for hd in range(B_HEADS):
        q = proj(C_QM + hd * MLA_QK, C_QM + (hd + 1) * MLA_QK)
        qm_ref[:, hd * MLA_QK:hd * MLA_QK + LANES] = (q[:, :LANES] * scale_b).astype(BF16)
        qm_ref[:, hd * MLA_QK + LANES:(hd + 1) * MLA_QK] = (rope(q[:, LANES:]) * scale_b).astype(BF16)

    ckv = proj(C_CKV, C_KR)
    cn = ((ckv * _rms_scale(ckv)) * gkv_ref[...]).astype(BF16)
    kr = rope(proj(C_KR, C_END)).astype(BF16)
    kn = jnp.dot(cn, wkv_ref[:, :B_HEADS * QK_NOPE], preferred_element_type=F32).astype(BF16)
    for hd in range(B_HEADS):
        km_ref[:, hd * MLA_QK:hd * MLA_QK + LANES] = kn[:, hd * QK_NOPE:(hd + 1) * QK_NOPE]
        km_ref[:, hd * MLA_QK + LANES:(hd + 1) * MLA_QK] = kr
    vm_ref[...] = jnp.dot(cn, wkv_ref[:, B_HEADS * QK_NOPE:], preferred_element_type=F32).astype(BF16)


def _in_proj(x2d, mod3, gmix, gkv, w_all, w_kvb, rope_a, rope_b, *, row_off, seq):
    t = x2d.shape[0]
    tm = 512
    pos_blocks = seq // tm
    tok = lambda width: pl.BlockSpec((tm, width), lambda i: (i, 0))
    return pl.pallas_call(
        _inproj_kernel,
        grid=(t // tm,),
        in_specs=[
            tok(D_MODEL),
            pl.BlockSpec((None, 6, D_MODEL), lambda i: (row_off + (i * tm) // seq, 0, 0)),
            _resident((1, D_MODEL)),
            _resident((1, KV_RANK)),
            _resident((D_MODEL, C_END)),
            _resident((KV_RANK, B_HEADS * (QK_NOPE + V_DIM))),
            pl.BlockSpec((tm, LANES), lambda i: (i % pos_blocks, 0)),
            pl.BlockSpec((tm, LANES), lambda i: (i % pos_blocks, 0)),
        ],
        out_specs=[tok(A_Q), tok(A_KV), tok(A_KV), tok(B_HEADS * MLA_QK), tok(B_HEADS * MLA_QK),
                   tok(B_HEADS * V_DIM)],
        out_shape=[
            jax.ShapeDtypeStruct((t, A_Q), BF16),
            jax.ShapeDtypeStruct((t, A_KV), BF16),
            jax.ShapeDtypeStruct((t, A_KV), BF16),
            jax.ShapeDtypeStruct((t, B_HEADS * MLA_QK), BF16),
            jax.ShapeDtypeStruct((t, B_HEADS * MLA_QK), BF16),
            jax.ShapeDtypeStruct((t, B_HEADS * V_DIM), BF16),
        ],
        compiler_params=_params(("parallel",)),
        name="in_proj",
    )(x2d, mod3, gmix, gkv, w_all, w_kvb, rope_a, rope_b)


def _bias_kernel(bucket_ref, rel_ref, o_ref):
    hd = pl.program_id(0)
    bucket = bucket_ref[...]
    acc = jnp.zeros(bucket.shape, F32)
    for b in range(N_BUCKETS):
        acc = jnp.where(bucket == b, rel_ref[b, hd], acc)
    o_ref[...] = acc


def _bias_table(bucket, rel_bias):
    return pl.pallas_call(
        _bias_kernel,
        grid=(A_HEADS,),
        in_specs=[
            pl.BlockSpec((BLOCK, 3 * BLOCK), lambda h: (0, 0)),
            pl.BlockSpec(memory_space=pltpu.SMEM),
        ],
        out_specs=pl.BlockSpec((None, BLOCK, 3 * BLOCK), lambda h: (h, 0, 0)),
        out_shape=jax.ShapeDtypeStruct((A_HEADS, BLOCK, 3 * BLOCK), F32),
        compiler_params=_params(("parallel",)),
        name="rel_bias_table",
    )(bucket, rel_bias)


def _win_kernel(q_ref, kp_ref, km_ref, kn_ref, vp_ref, vm_ref, vn_ref, bias_ref, sink_ref, o_ref, *, seq, tq):
    nsub = tq // BLOCK
    rows = A_GROUP * BLOCK
    first_block = pl.program_id(2) * nsub
    q_off = lax.broadcasted_iota(jnp.int32, (rows, 3 * BLOCK), 0) & (BLOCK - 1)
    k_off = lax.broadcasted_iota(jnp.int32, (rows, 3 * BLOCK), 1) - BLOCK
    in_window = jnp.abs(k_off - q_off) <= WINDOW
    bias = bias_ref[...]
    sink = sink_ref[...]

    def band(prev_ref, main_ref, next_ref, j):
        lo = prev_ref[...] if j == 0 else main_ref[(j - 1) * BLOCK:j * BLOCK, :]
        hi = next_ref[...] if j == nsub - 1 else main_ref[(j + 1) * BLOCK:(j + 2) * BLOCK, :]
        return jnp.concatenate([lo, main_ref[j * BLOCK:(j + 1) * BLOCK, :], hi], axis=0)

    for j in range(nsub):
        qs = jnp.concatenate(
            [q_ref[j * BLOCK:(j + 1) * BLOCK, r * HEAD_DIM:(r + 1) * HEAD_DIM] for r in range(A_GROUP)], axis=0)
        kb = band(kp_ref, km_ref, kn_ref, j)
        vb = band(vp_ref, vm_ref, vn_ref, j)
        s = lax.dot_general(qs, kb, (((1,), (1,)), ((), ())), preferred_element_type=F32)
        k_abs = (first_block + j) * BLOCK + k_off
        valid = in_window & (k_abs >= 0) & (k_abs < seq)
        s = jnp.where(valid, s + bias, NEG)
        m = jnp.maximum(jnp.max(s, axis=-1, keepdims=True), sink)
        p = jnp.exp(s - jnp.tile(m, (1, 3 * BLOCK // LANES)))
        denom = jnp.sum(p, axis=-1, keepdims=True) + jnp.exp(sink - m)
        o = jnp.dot(p.astype(BF16), vb, preferred_element_type=F32) / denom
        for r in range(A_GROUP):
            o_ref[j * BLOCK:(j + 1) * BLOCK, r * HEAD_DIM:(r + 1) * HEAD_DIM] = (
                o[r * BLOCK:(r + 1) * BLOCK, :].astype(BF16))


def _win_attn(qa, ka, va, bias2, sink2, *, seq, nbatch):
    t = qa.shape[0]
    tq = 512
    nsub = tq // BLOCK
    qt = seq // tq
    last = t // BLOCK - 1
    main = lambda width: pl.BlockSpec((tq, width), lambda b, g, i: (b * qt + i, g))
    prev = pl.BlockSpec((BLOCK, HEAD_DIM), lambda b, g, i: (jnp.maximum((b * qt + i) * nsub - 1, 0), g))
    nxt = pl.BlockSpec((BLOCK, HEAD_DIM), lambda b, g, i: (jnp.minimum((b * qt + i + 1) * nsub, last), g))
    rows = A_GROUP * BLOCK
    return pl.pallas_call(
        functools.partial(_win_kernel, seq=seq, tq=tq),
        grid=(nbatch, A_KV_HEADS, qt),
        in_specs=[
            main(A_GROUP * HEAD_DIM),
            prev, main(HEAD_DIM), nxt,
            prev, main(HEAD_DIM), nxt,
            pl.BlockSpec((None, rows, 3 * BLOCK), lambda b, g, i: (g, 0, 0)),
            pl.BlockSpec((None, rows, LANES), lambda b, g, i: (g, 0, 0)),
        ],
        out_specs=main(A_GROUP * HEAD_DIM),
        out_shape=jax.ShapeDtypeStruct((t, A_Q), BF16),
        compiler_params=_params(("parallel", "parallel", "parallel")),
        name="win_attn",
    )(qa, ka, ka, ka, va, va, va, bias2, sink2)


def _mla_kernel(q_ref, k_ref, v_ref, o_ref, s_ref, p_ref, a_ref, m_ref, l_ref, acc_ref, *, seq, tk):
    n = seq // tk

    def scores(c, slot):
        k = k_ref[pl.ds(pl.multiple_of(c * tk, tk), tk), :]
        s_ref[slot] = lax.dot_general(q_ref[...], k, (((1,), (1,)), ((), ())), preferred_element_type=F32)

    def softmax(slot):
        m_prev = m_ref[...]
        m_new = jnp.maximum(m_prev, jnp.max(s_ref[slot], axis=-1, keepdims=True))
        alpha = jnp.exp2(m_prev - m_new)
        p = jnp.exp2(s_ref[slot] - jnp.tile(m_new, (1, tk // LANES)))
        l_ref[...] = alpha * l_ref[...] + jnp.sum(p, axis=-1, keepdims=True)
        p_ref[slot] = p.astype(BF16)
        a_ref[slot] = alpha
        m_ref[...] = m_new

    def values(c, slot):
        v = v_ref[pl.ds(pl.multiple_of(c * tk, tk), tk), :]
        acc_ref[...] = a_ref[slot] * acc_ref[...] + jnp.dot(p_ref[slot], v, preferred_element_type=F32)

    m_ref[...] = jnp.full(m_ref.shape, -jnp.inf, F32)
    l_ref[...] = jnp.zeros(l_ref.shape, F32)
    acc_ref[...] = jnp.zeros(acc_ref.shape, F32)
    p_ref[1] = jnp.zeros(p_ref.shape[1:], BF16)
    a_ref[1] = jnp.ones(a_ref.shape[1:], F32)
    scores(0, 0)

    def pair(i, carry):
        c = 2 * i
        scores(c + 1, 1)
        softmax(0)
        values(jnp.maximum(c - 1, 0), 1)
        scores(c + 2, 0)
        softmax(1)
        values(c, 0)
        return carry

    lax.fori_loop(0, n // 2 - 1, pair, 0)
    scores(n - 1, 1)
    softmax(0)
    values(max(n - 3, 0), 1)
    softmax(1)
    values(n - 2, 0)
    values(n - 1, 1)
    o_ref[...] = (acc_ref[...] / l_ref[...]).astype(BF16)


def _mla_attn(qm, km, vm, *, seq, nbatch):
    t = qm.shape[0]
    tq = 1024
    tk = 512
    assert (seq // tk) % 2 == 0
    qt = seq // tq
    return pl.pallas_call(
        functools.partial(_mla_kernel, seq=seq, tk=tk),
        grid=(nbatch, B_HEADS, qt),
        in_specs=[
            pl.BlockSpec((tq, MLA_QK), lambda b, h, i: (b * qt + i, h)),
            pl.BlockSpec((seq, MLA_QK), lambda b, h, i: (b, h)),
            pl.BlockSpec((seq, V_DIM), lambda b, h, i: (b, h)),
        ],
        out_specs=pl.BlockSpec((tq, V_DIM), lambda b, h, i: (b * qt + i, h)),
        out_shape=jax.ShapeDtypeStruct((t, B_HEADS * V_DIM), BF16),
        scratch_shapes=[
            pltpu.VMEM((2, tq, tk), F32),
            pltpu.VMEM((2, tq, tk), BF16),
            pltpu.VMEM((2, tq, LANES), F32),
        ] + [pltpu.VMEM((tq, LANES), F32)] * 3,
        compiler_params=_params(("parallel", "parallel", "parallel")),
        name="mla_attn",
    )(qm, km, vm)


def _outproj_kernel(x_ref, mod_ref, oa_ref, ob_ref, wo_ref, gmlp_ref, x1_ref, h2_ref):
    mix = (jnp.dot(oa_ref[...], wo_ref[:A_Q, :], preferred_element_type=F32)
           + jnp.dot(ob_ref[...], wo_ref[A_Q:, :], preferred_element_type=F32))
    x1 = x_ref[...] + mod_ref[2:3, :] * mix
    x1_ref[...] = x1
    y = (x1 * _rms_scale(x1)) * gmlp_ref[...]
    h2_ref[...] = (y * (1.0 + mod_ref[4:5, :]) + mod_ref[3:4, :]).astype(BF16)


def _out_proj(x2d, mod3, out_a, out_b, w_o, gmlp, *, row_off, seq):
    t = x2d.shape[0]
    tm = 512
    tok = lambda width: pl.BlockSpec((tm, width), lambda i: (i, 0))
    return pl.pallas_call(
        _outproj_kernel,
        grid=(t // tm,),
        in_specs=[
            tok(D_MODEL),
            pl.BlockSpec((None, 6, D_MODEL), lambda i: (row_off + (i * tm) // seq, 0, 0)),
            tok(A_Q),
            tok(B_HEADS * V_DIM),
            _resident((A_Q + B_HEADS * V_DIM, D_MODEL)),
            _resident((1, D_MODEL)),
        ],
        out_specs=[tok(D_MODEL), tok(D_MODEL)],
        out_shape=[jax.ShapeDtypeStruct((t, D_MODEL), F32), jax.ShapeDtypeStruct((t, D_MODEL), BF16)],
        compiler_params=_params(("parallel",)),
        name="out_proj",
    )(x2d, mod3, out_a, out_b, w_o, gmlp)


def _mlp_kernel(h2_ref, x1_ref, mod_ref, w1_ref, w2_ref, gfin_ref, o_ref, *, final_norm):
    j = pl.program_id(1)

    @pl.when(j == 0)
    def _():
        o_ref[...] = jnp.zeros(o_ref.shape, F32)

    u = jnp.maximum(jnp.dot(h2_ref[...], w1_ref[...], preferred_element_type=F32), 0.0)
    o_ref[...] += jnp.dot((u * u).astype(BF16), w2_ref[...], preferred_element_type=F32)

    @pl.when(j == pl.num_programs(1) - 1)
    def _():
        x2 = x1_ref[...] + mod_ref[5:6, :] * o_ref[...]
        if final_norm:
            x2 = (x2 * _rms_scale(x2)) * gfin_ref[...]
        o_ref[...] = x2


def _mlp(h2, x1, mod3, w1, w2, gfin, *, row_off, seq, final_norm):
    t = h2.shape[0]
    tm = 512
    fc = 512
    tok = lambda width: pl.BlockSpec((tm, width), lambda i, j: (i, 0))
    return pl.pallas_call(
        functools.partial(_mlp_kernel, final_norm=final_norm),
        grid=(t // tm, D_FF // fc),
        in_specs=[
            tok(D_MODEL),
            tok(D_MODEL),
            pl.BlockSpec((None, 6, D_MODEL), lambda i, j: (row_off + (i * tm) // seq, 0, 0)),
            pl.BlockSpec((D_MODEL, fc), lambda i, j: (0, j)),
            pl.BlockSpec((fc, D_MODEL), lambda i, j: (j, 0)),
            pl.BlockSpec((1, D_MODEL), lambda i, j: (0, 0)),
        ],
        out_specs=tok(D_MODEL),
        out_shape=jax.ShapeDtypeStruct((t, D_MODEL), F32),
        compiler_params=_params(("parallel", "arbitrary")),
        name="mlp",
    )(h2, x1, mod3, w1, w2, gfin)


def _t5_bucket(rel):
    half = N_BUCKETS // 2
    max_exact = half // 2
    ret = jnp.where(rel > 0, half, 0)
    n = jnp.abs(rel)
    nf = jnp.maximum(n, 1).astype(jnp.float32)
    large = max_exact + (jnp.log(nf / max_exact) / math.log(MAX_DISTANCE / max_exact)
                         * (half - max_exact)).astype(jnp.int32)
    large = jnp.minimum(large, half - 1)
    return ret + jnp.where(n < max_exact, n, large)


def _bucket_table():
    q_off = jnp.arange(BLOCK)[:, None]
    k_off = jnp.arange(3 * BLOCK)[None, :] - BLOCK
    return _t5_bucket(k_off - q_off).astype(jnp.int32)


def _rope_tables(seq):
    half = QK_ROPE // 2
    inv = ROPE_THETA ** (-jnp.arange(half, dtype=jnp.float32) / half)
    ang = jnp.arange(seq).astype(jnp.float32)[:, None] * inv[None, :]
    cos, sin = jnp.cos(ang), jnp.sin(ang)
    pad = jnp.zeros((seq, LANES - QK_ROPE), F32)
    return (jnp.concatenate([cos, cos, pad], axis=1), jnp.concatenate([-sin, sin, pad], axis=1))


def _swap_halves(w):
    half = QK_ROPE // 2
    return jnp.concatenate([w[..., half:], w[..., :half]], axis=-1)


def _arrange_w_in(w):
    d = w.shape[0]
    o = np.cumsum([0, A_Q, A_KV, A_KV, B_QN, B_QR, KV_RANK, QK_ROPE]).tolist()
    wqn = w[:, o[3]:o[4]].reshape(d, B_HEADS, QK_NOPE)
    wqr = w[:, o[4]:o[5]].reshape(d, B_HEADS, QK_ROPE)
    wqm = jnp.concatenate([wqn, wqr, _swap_halves(wqr)], axis=-1).reshape(d, B_HEADS * MLA_QK)
    wkr = w[:, o[6]:o[7]]
    return jnp.concatenate([w[:, :o[3]], wqm, w[:, o[5]:o[6]], wkr, _swap_halves(wkr)], axis=1).astype(BF16)


def _arrange_w_kvb(w):
    w3 = w.reshape(KV_RANK, B_HEADS, QK_NOPE + V_DIM)
    return jnp.concatenate([w3[..., :QK_NOPE].reshape(KV_RANK, -1), w3[..., QK_NOPE:].reshape(KV_RANK, -1)],
                           axis=1).astype(BF16)


def _trunk(x, mod3, row_off, layers, bias2, g_final):
    nbatch, seq, _ = x.shape
    x2d = x.reshape(nbatch * seq, D_MODEL)
    rope_a, rope_b = _rope_tables(seq)
    for l, p in enumerate(layers):
        kw = dict(row_off=row_off, seq=seq)
        qa, ka, va, qm, km, vm = _in_proj(x2d, mod3[l], p["g_mix"], p["g_kv"], p["w_in"], p["w_kv_b"],
                                          rope_a, rope_b, **kw)
        out_a = _win_attn(qa, ka, va, bias2, p["sink"], seq=seq, nbatch=nbatch)
        out_b = _mla_attn(qm, km, vm, seq=seq, nbatch=nbatch)
        x1, h2 = _out_proj(x2d, mod3[l], out_a, out_b, p["w_o"], p["g_mlp"], **kw)
        x2d = _mlp(h2, x1, mod3[l], p["w_ff1"], p["w_ff2"], g_final, final_norm=(l == len(layers) - 1), **kw)
    return x2d.reshape(nbatch, seq, D_MODEL)


def kernel(x_prompt, x_sample, c_prompt, c_sample, w_ada, b_ada, g_mix, w_in, sink, g_kv, w_kv_b, w_o, g_mlp,
           w_ff1, w_ff2, rel_bias, g_final):
    n_prompt, n_sample = c_prompt.shape[0], c_sample.shape[0]
    rows = -(-(n_prompt + n_sample) // 16) * 16
    c_all = jnp.concatenate(
        [c_prompt, c_sample, jnp.zeros((rows - n_prompt - n_sample, D_MODEL), F32)], axis=0)

    depth = w_in.shape[0]
    mod3, layers = [], []
    for l in range(depth):
        mod = _ada_mod(c_all, w_ada[l], b_ada[l].reshape(1, -1))
        mod3.append(mod.reshape(rows, 6, D_MODEL))
        sink_rows = jnp.repeat(sink[l].astype(F32), BLOCK).reshape(A_KV_HEADS, A_GROUP * BLOCK, 1)
        layers.append(dict(
            g_mix=g_mix[l].reshape(1, -1), g_kv=g_kv[l].reshape(1, -1), g_mlp=g_mlp[l].reshape(1, -1),
            w_in=_arrange_w_in(w_in[l]), w_kv_b=_arrange_w_kvb(w_kv_b[l]), w_o=w_o[l].astype(BF16),
            w_ff1=w_ff1[l].astype(BF16), w_ff2=w_ff2[l].astype(BF16),
            sink=jnp.broadcast_to(sink_rows, (A_KV_HEADS, A_GROUP * BLOCK, LANES)),
        ))
    bias2 = _bias_table(_bucket_table(), rel_bias.astype(F32)).reshape(A_KV_HEADS, A_GROUP * BLOCK, 3 * BLOCK)
    gfin = g_final.reshape(1, -1)

    y_prompt = _trunk(x_prompt, mod3, 0, layers, bias2, gfin)
    y_sample = _trunk(x_sample, mod3, n_prompt, layers, bias2, gfin)
    return (y_prompt, y_sample)
```

```python
import functools
import math

import jax
import jax.numpy as jnp
import numpy as np
from jax import lax
from jax.experimental import pallas as pl
from jax.experimental.pallas import tpu as pltpu

D_MODEL = 2048
DEPTH = 1
HEAD_DIM = 128
A_HEADS = 8
A_KV_HEADS = 2
A_GROUP = A_HEADS // A_KV_HEADS
WINDOW = 128
BLOCK = 128
N_BUCKETS = 32
MAX_DISTANCE = 128
B_HEADS = 8
KV_RANK = 512
QK_NOPE = 128
QK_ROPE = 64
V_DIM = 128
ROPE_THETA = 10000.0
D_FF = 4 * D_MODEL
EPS = 1e-6
NEG = -1e30

A_Q = A_HEADS * HEAD_DIM
A_KV = A_KV_HEADS * HEAD_DIM
B_QN = B_HEADS * QK_NOPE
B_QR = B_HEADS * QK_ROPE

LANES = 128
BF16_ROWS = 16
MXU_COLS = 256
MLA_QK = 2 * LANES
V_SLAB = 512
MLA_TK = V_SLAB
MLA_LAG = 1
MLA_SLOTS = 2 * MLA_LAG
MLA_UNROLL = 4
VMEM_LIMIT = 56 * 1024 * 1024

C_QA = 0
C_KA = C_QA + A_Q
C_VA = C_KA + A_KV
C_QN = C_VA + A_KV
C_QR = C_QN + B_QN
C_CKV = C_QR + B_QR
C_KR = C_CKV + KV_RANK
C_END = C_KR + QK_ROPE
X_QRS = 2 * QK_ROPE
X_END = X_QRS + B_QR

F32 = jnp.float32
BF16 = jnp.bfloat16


def _params(sem, flags=None):
    return pltpu.CompilerParams(dimension_semantics=sem, vmem_limit_bytes=VMEM_LIMIT, flags=flags)


def _resident(shape):
    return pl.BlockSpec(shape, lambda *_: (0,) * len(shape), pipeline_mode=pl.Buffered(1))


def _rms_scale(x):
    return lax.rsqrt(jnp.mean(x * x, axis=-1, keepdims=True) + EPS)


def _ada_kernel(c_ref, w_ref, b_ref, o_ref):
    c = c_ref[...]
    h = c * jax.nn.sigmoid(c)
    o_ref[...] = jnp.dot(h.astype(BF16), w_ref[...].astype(BF16), preferred_element_type=F32) + b_ref[...]


def _ada_mod(c_all, w, b):
    rows, n = c_all.shape[0], w.shape[1]
    tn = 1024
    return pl.pallas_call(
        _ada_kernel,
        grid=(n // tn,),
        in_specs=[
            pl.BlockSpec((rows, D_MODEL), lambda j: (0, 0)),
            pl.BlockSpec((D_MODEL, tn), lambda j: (0, j)),
            pl.BlockSpec((1, tn), lambda j: (0, j)),
        ],
        out_specs=pl.BlockSpec((rows, tn), lambda j: (0, j)),
        out_shape=jax.ShapeDtypeStruct((rows, n), F32),
        compiler_params=_params(("parallel",)),
        name="ada_mod",
    )(c_all, w, b)


def _inproj_kernel(x_ref, mod_ref, gmix_ref, gkv_ref, w_ref, wx_ref, wk_ref, wvt_ref, ra_ref, rb_ref,
                   qa_ref, ka_ref, va_ref, qm_ref, km_ref, vt_ref):
    x = x_ref[...]
    y = (x * _rms_scale(x)) * gmix_ref[...]
    h = (y * (1.0 + mod_ref[1:2, :]) + mod_ref[0:1, :]).astype(BF16)

    def proj(lo, hi, ref=w_ref):
        return jnp.dot(h, ref[:, lo:hi], preferred_element_type=F32)

    qa_ref[...] = (proj(C_QA, C_KA) * (HEAD_DIM ** -0.5)).astype(BF16)
    ka_ref[...] = proj(C_KA, C_VA).astype(BF16)
    va_ref[...] = proj(C_VA, C_QN).astype(BF16)

    ra = ra_ref[...]
    rb = rb_ref[...]
    ra2 = ra + pltpu.roll(ra, QK_ROPE, 1)
    rb2 = rb + pltpu.roll(rb, QK_ROPE, 1)
    low_half = lax.broadcasted_iota(jnp.int32, ra.shape, 1) < QK_ROPE

    scale_b = (QK_NOPE + QK_ROPE) ** -0.5 * math.log2(math.e)
    qn = proj(C_QN, C_QR)
    qr = proj(C_QR, C_CKV)
    qrs = proj(X_QRS, X_END, wx_ref)
    for pair in range(B_HEADS // 2):
        cols = slice(pair * LANES, (pair + 1) * LANES)
        roped = qr[:, cols] * ra2 + qrs[:, cols] * rb2
        for hd, blk in ((2 * pair, roped), (2 * pair + 1, pltpu.roll(roped, QK_ROPE, 1))):
            qm_ref[:, hd * MLA_QK:hd * MLA_QK + LANES] = (
                qn[:, hd * QK_NOPE:(hd + 1) * QK_NOPE] * scale_b).astype(BF16)
            qm_ref[:, hd * MLA_QK + LANES:(hd + 1) * MLA_QK] = (
                jnp.where(low_half, blk, 0.0) * scale_b).astype(BF16)

    ckv = proj(C_CKV, C_KR)
    cn = ((ckv * _rms_scale(ckv)) * gkv_ref[...]).astype(BF16)
    kr_blk = proj(0, X_QRS, wx_ref)
    kr = (kr_blk * ra + pltpu.roll(kr_blk, QK_ROPE, 1) * rb).astype(BF16)
    kn = jnp.dot(cn, wk_ref[...], preferred_element_type=F32).astype(BF16)
    for hd in range(B_HEADS):
        km_ref[:, hd * MLA_QK:hd * MLA_QK + LANES] = kn[:, hd * QK_NOPE:(hd + 1) * QK_NOPE]
        km_ref[:, hd * MLA_QK + LANES:(hd + 1) * MLA_QK] = kr
    vt_ref[...] = lax.dot_general(wvt_ref[...], cn, (((1,), (1,)), ((), ())),
                                  preferred_element_type=F32).astype(BF16)


def _in_proj(x2d, mod3, gmix, gkv, w_all, w_ext, w_k, w_vt, rope_a, rope_b, *, row_off, seq):
    t = x2d.shape[0]
    tm = V_SLAB
    pos_blocks = seq // tm
    tok = lambda width: pl.BlockSpec((tm, width), lambda i: (i, 0))
    return pl.pallas_call(
        _inproj_kernel,
        grid=(t // tm,),
        in_specs=[
            tok(D_MODEL),
            pl.BlockSpec((None, 6, D_MODEL), lambda i: (row_off + (i * tm) // seq, 0, 0)),
            _resident((1, D_MODEL)),
            _resident((1, KV_RANK)),
            _resident((D_MODEL, C_END)),
            _resident((D_MODEL, X_END)),
            _resident((KV_RANK, B_HEADS * QK_NOPE)),
            _resident((B_HEADS * V_DIM, KV_RANK)),
            pl.BlockSpec((tm, LANES), lambda i: (i % pos_blocks, 0)),
            pl.BlockSpec((tm, LANES), lambda i: (i % pos_blocks, 0)),
        ],
        out_specs=[tok(A_Q), tok(A_KV), tok(A_KV), tok(B_HEADS * MLA_QK), tok(B_HEADS * MLA_QK),
                   pl.BlockSpec((None, B_HEADS * V_DIM, tm), lambda i: (i, 0, 0))],
        out_shape=[
            jax.ShapeDtypeStruct((t, A_Q), BF16),
            jax.ShapeDtypeStruct((t, A_KV), BF16),
            jax.ShapeDtypeStruct((t, A_KV), BF16),
            jax.ShapeDtypeStruct((t, B_HEADS * MLA_QK), BF16),
            jax.ShapeDtypeStruct((t, B_HEADS * MLA_QK), BF16),
            jax.ShapeDtypeStruct((t // tm, B_HEADS * V_DIM, tm), BF16),
        ],
        compiler_params=_params(("parallel",)),
        name="in_proj",
    )(x2d, mod3, gmix, gkv, w_all, w_ext, w_k, w_vt, rope_a, rope_b)


def _bias_kernel(bucket_ref, rel_ref, o_ref):
    hd = pl.program_id(0)
    bucket = bucket_ref[...]
    acc = jnp.zeros(bucket.shape, F32)
    for b in range(N_BUCKETS):
        acc = jnp.where(bucket == b, rel_ref[b, hd], acc)
    o_ref[...] = acc


def _bias_table(bucket, rel_bias):
    return pl.pallas_call(
        _bias_kernel,
        grid=(A_HEADS,),
        in_specs=[
            pl.BlockSpec((BLOCK, 3 * BLOCK), lambda h: (0, 0)),
            pl.BlockSpec(memory_space=pltpu.SMEM),
        ],
        out_specs=pl.BlockSpec((None, BLOCK, 3 * BLOCK), lambda h: (h, 0, 0)),
        out_shape=jax.ShapeDtypeStruct((A_HEADS, BLOCK, 3 * BLOCK), F32),
        compiler_params=_params(("parallel",)),
        name="rel_bias_table",
    )(bucket, rel_bias)


def _win_kernel(q_ref, kp_ref, km_ref, kn_ref, vp_ref, vm_ref, vn_ref, bias_ref, win_ref, sink_ref, o_ref, *,
                seq, tq):
    nsub = tq // BLOCK
    rows = A_GROUP * BLOCK
    first_block = pl.program_id(2) * nsub
    k_off = lax.broadcasted_iota(jnp.int32, (1, 3 * BLOCK), 1) - BLOCK
    in_window = win_ref[...] > 0.5
    bias = bias_ref[...]
    sink = sink_ref[...]

    def band(prev_ref, main_ref, next_ref, j):
        lo = prev_ref[...] if j == 0 else main_ref[(j - 1) * BLOCK:j * BLOCK, :]
        hi = next_ref[...] if j == nsub - 1 else main_ref[(j + 1) * BLOCK:(j + 2) * BLOCK, :]
        return jnp.concatenate([lo, main_ref[j * BLOCK:(j + 1) * BLOCK, :], hi], axis=0)

    for j in range(nsub):
        qs = jnp.concatenate(
            [q_ref[j * BLOCK:(j + 1) * BLOCK, r * HEAD_DIM:(r + 1) * HEAD_DIM] for r in range(A_GROUP)], axis=0)
        kb = band(kp_ref, km_ref, kn_ref, j)
        vb = band(vp_ref, vm_ref, vn_ref, j)
        s = lax.dot_general(qs, kb, (((1,), (1,)), ((), ())), preferred_element_type=F32)
        k_abs = (first_block + j) * BLOCK + k_off
        in_seq = (k_abs >= 0) & (k_abs < seq)
        s = jnp.where(in_window & in_seq, s + bias, NEG)
        m = jnp.maximum(jnp.max(s, axis=-1, keepdims=True), sink)
        p = jnp.exp(s - jnp.tile(m, (1, 3 * BLOCK // LANES)))
        denom = jnp.sum(p, axis=-1, keepdims=True) + jnp.exp(sink - m)
        o = jnp.dot(p.astype(BF16), vb, preferred_element_type=F32) / denom
        for r in range(A_GROUP):
            o_ref[j * BLOCK:(j + 1) * BLOCK, r * HEAD_DIM:(r + 1) * HEAD_DIM] = (
                o[r * BLOCK:(r + 1) * BLOCK, :].astype(BF16))


def _win_attn(qa, ka, va, bias2, sink2, *, seq, nbatch):
    t = qa.shape[0]
    tq = 512
    nsub = tq // BLOCK
    qt = seq // tq
    last = t // BLOCK - 1
    main = lambda width: pl.BlockSpec((tq, width), lambda b, g, i: (b * qt + i, g))
    prev = pl.BlockSpec((BLOCK, HEAD_DIM), lambda b, g, i: (jnp.maximum((b * qt + i) * nsub - 1, 0), g))
    nxt = pl.BlockSpec((BLOCK, HEAD_DIM), lambda b, g, i: (jnp.minimum((b * qt + i + 1) * nsub, last), g))
    rows = A_GROUP * BLOCK
    q_off = np.arange(rows)[:, None] % BLOCK
    k_off = np.arange(3 * BLOCK)[None, :] - BLOCK
    win = jnp.asarray(np.abs(k_off - q_off) <= WINDOW, F32)
    return pl.pallas_call(
        functools.partial(_win_kernel, seq=seq, tq=tq),
        grid=(nbatch, A_KV_HEADS, qt),
        in_specs=[
            main(A_GROUP * HEAD_DIM),
            prev, main(HEAD_DIM), nxt,
            prev, main(HEAD_DIM), nxt,
            pl.BlockSpec((None, rows, 3 * BLOCK), lambda b, g, i: (g, 0, 0)),
            pl.BlockSpec((rows, 3 * BLOCK), lambda b, g, i: (0, 0)),
            pl.BlockSpec((None, rows, LANES), lambda b, g, i: (g, 0, 0)),
        ],
        out_specs=main(A_GROUP * HEAD_DIM),
        out_shape=jax.ShapeDtypeStruct((t, A_Q), BF16),
        compiler_params=_params(("parallel", "parallel", "parallel")),
        name="win_attn",
    )(qa, ka, ka, ka, va, va, va, bias2, win, sink2)


def _mla_kernel(q_ref, k_ref, vt_ref, o_ref, s_ref, cmax_ref, m_ref, acc_ref, *, seq, tk, unroll):
    n = seq // tk
    tq = q_ref.shape[0]
    LAG, SLOTS = MLA_LAG, MLA_SLOTS

    def scores(c, slot):
        k = k_ref[pl.ds(pl.multiple_of(c * tk, tk), tk), :]
        s = lax.dot_general(k, q_ref[...], (((1,), (1,)), ((), ())), preferred_element_type=F32)
        s_ref[slot] = s
        cmax_ref[slot] = jnp.max(s, axis=0, keepdims=True)

    ones = jnp.ones((acc_ref.shape[0] - V_DIM, tk), BF16)

    def attend(c, slot):
        m_prev = m_ref[...]
        m_new = jnp.maximum(m_prev, cmax_ref[slot])
        alpha = jnp.exp2(m_prev - m_new)
        m_ref[...] = m_new
        slabs = tk // V_SLAB
        vt = jnp.concatenate([vt_ref[c * slabs + i] for i in range(slabs)], axis=1)
        lhs = jnp.concatenate([vt, ones], axis=0)
        for lo in range(0, tq, MXU_COLS):
            cols = slice(lo, lo + MXU_COLS)
            p = jnp.exp2(s_ref[slot, :, cols] - m_new[:, cols]).astype(BF16)
            acc_ref[:, cols] = alpha[:, cols] * acc_ref[:, cols] + jnp.dot(lhs, p, preferred_element_type=F32)

    m_ref[...] = jnp.full(m_ref.shape, -jnp.inf, F32)
    acc_ref[...] = jnp.zeros(acc_ref.shape, F32)
    for t in range(LAG):
        scores(t, t)

    def step(t, j, with_scores):
        if with_scores:
            scores(t + LAG, (j + LAG) % SLOTS)
        attend(t, j)

    def trip(i, carry):
        for j in range(unroll):
            step(i * unroll + j, j % SLOTS, True)
        return carry

    lax.fori_loop(0, n // unroll - 1, trip, 0)
    for j in range(unroll):
        step(n - unroll + j, j % SLOTS, j + LAG < unroll)
    o_ref[...] = (acc_ref[:V_DIM, :] / acc_ref[V_DIM:V_DIM + 1, :]).T.astype(BF16)


def _mla_attn(qm, km, vt, *, seq, nbatch):
    t = qm.shape[0]
    tk = MLA_TK
    n = seq // tk
    unroll = min(MLA_UNROLL, n)
    assert n % unroll == 0 and unroll % MLA_SLOTS == 0
    tq = 2048 if n <= unroll else 1024
    qt = seq // tq
    return pl.pallas_call(
        functools.partial(_mla_kernel, seq=seq, tk=tk, unroll=unroll),
        grid=(nbatch, B_HEADS, qt),
        in_specs=[
            pl.BlockSpec((tq, MLA_QK), lambda b, h, i: (b * qt + i, h)),
            pl.BlockSpec((seq, MLA_QK), lambda b, h, i: (b, h)),
            pl.BlockSpec((seq // V_SLAB, V_DIM, V_SLAB), lambda b, h, i: (b, h, 0)),
        ],
        out_specs=pl.BlockSpec((tq, V_DIM), lambda b, h, i: (b * qt + i, h)),
        out_shape=jax.ShapeDtypeStruct((t, B_HEADS * V_DIM), BF16),
        scratch_shapes=[
            pltpu.VMEM((MLA_SLOTS, tk, tq), F32),
            pltpu.VMEM((MLA_SLOTS, 1, tq), F32),
            pltpu.VMEM((1, tq), F32),
            pltpu.VMEM((V_DIM + BF16_ROWS, tq), F32),
        ],
        compiler_params=_params(("parallel", "parallel", "parallel")),
        name="mla_attn",
    )(qm, km, vt)


def _outproj_kernel(x_ref, mod_ref, oa_ref, ob_ref, wo_ref, gmlp_ref, x1_ref, h2_ref):
    mix = (jnp.dot(oa_ref[...], wo_ref[:A_Q, :], preferred_element_type=F32)
           + jnp.dot(ob_ref[...], wo_ref[A_Q:, :], preferred_element_type=F32))
    x1 = x_ref[...] + mod_ref[2:3, :] * mix
    x1_ref[...] = x1
    y = (x1 * _rms_scale(x1)) * gmlp_ref[...]
    h2_ref[...] = (y * (1.0 + mod_ref[4:5, :]) + mod_ref[3:4, :]).astype(BF16)


def _out_proj(x2d, mod3, out_a, out_b, w_o, gmlp, *, row_off, seq):
    t = x2d.shape[0]
    tm = 512
    tok = lambda width: pl.BlockSpec((tm, width), lambda i: (i, 0))
    return pl.pallas_call(
        _outproj_kernel,
        grid=(t // tm,),
        in_specs=[
            tok(D_MODEL),
            pl.BlockSpec((None, 6, D_MODEL), lambda i: (row_off + (i * tm) // seq, 0, 0)),
            tok(A_Q),
            tok(B_HEADS * V_DIM),
            _resident((A_Q + B_HEADS * V_DIM, D_MODEL)),
            _resident((1, D_MODEL)),
        ],
        out_specs=[tok(D_MODEL), tok(D_MODEL)],
        out_shape=[jax.ShapeDtypeStruct((t, D_MODEL), F32), jax.ShapeDtypeStruct((t, D_MODEL), BF16)],
        compiler_params=_params(("parallel",)),
        name="out_proj",
    )(x2d, mod3, out_a, out_b, w_o, gmlp)


def _mlp_kernel(h2_ref, x1_ref, mod_ref, w1_ref, w2_ref, gfin_ref, o_ref, *, final_norm):
    j = pl.program_id(1)

    @pl.when(j == 0)
    def _():
        o_ref[...] = jnp.zeros(o_ref.shape, F32)

    u = jnp.maximum(jnp.dot(h2_ref[...], w1_ref[...], preferred_element_type=F32), 0.0)
    o_ref[...] += jnp.dot((u * u).astype(BF16), w2_ref[...], preferred_element_type=F32)

    @pl.when(j == pl.num_programs(1) - 1)
    def _():
        x2 = x1_ref[...] + mod_ref[5:6, :] * o_ref[...]
        if final_norm:
            x2 = (x2 * _rms_scale(x2)) * gfin_ref[...]
        o_ref[...] = x2


def _mlp(h2, x1, mod3, w1, w2, gfin, *, row_off, seq, final_norm):
    t = h2.shape[0]
    tm = 512
    fc = 1024
    tok = lambda width: pl.BlockSpec((tm, width), lambda i, j: (i, 0))
    return pl.pallas_call(
        functools.partial(_mlp_kernel, final_norm=final_norm),
        grid=(t // tm, D_FF // fc),
        in_specs=[
            tok(D_MODEL),
            tok(D_MODEL),
            pl.BlockSpec((None, 6, D_MODEL), lambda i, j: (row_off + (i * tm) // seq, 0, 0)),
            pl.BlockSpec((D_MODEL, fc), lambda i, j: (0, j)),
            pl.BlockSpec((fc, D_MODEL), lambda i, j: (j, 0)),
            pl.BlockSpec((1, D_MODEL), lambda i, j: (0, 0)),
        ],
        out_specs=tok(D_MODEL),
        out_shape=jax.ShapeDtypeStruct((t, D_MODEL), F32),
        compiler_params=_params(("parallel", "arbitrary")),
        name="mlp",
    )(h2, x1, mod3, w1, w2, gfin)


def _t5_bucket(rel):
    half = N_BUCKETS // 2
    max_exact = half // 2
    ret = jnp.where(rel > 0, half, 0)
    n = jnp.abs(rel)
    nf = jnp.maximum(n, 1).astype(jnp.float32)
    large = max_exact + (jnp.log(nf / max_exact) / math.log(MAX_DISTANCE / max_exact)
                         * (half - max_exact)).astype(jnp.int32)
    large = jnp.minimum(large, half - 1)
    return ret + jnp.where(n < max_exact, n, large)


def _bucket_table():
    q_off = jnp.arange(BLOCK)[:, None]
    k_off = jnp.arange(3 * BLOCK)[None, :] - BLOCK
    return _t5_bucket(k_off - q_off).astype(jnp.int32)


def _rope_tables(seq):
    half = QK_ROPE // 2
    inv = ROPE_THETA ** (-jnp.arange(half, dtype=jnp.float32) / half)
    ang = jnp.arange(seq).astype(jnp.float32)[:, None] * jnp.tile(inv, LANES // half)[None, :]
    lane = np.arange(LANES)
    keep = jnp.asarray(lane < QK_ROPE, F32)[None, :]
    sign = jnp.asarray(np.where(lane % QK_ROPE < half, -1.0, 1.0) * (lane < QK_ROPE), F32)[None, :]
    return jnp.cos(ang) * keep, jnp.sin(ang) * sign


def _swap_halves(w):
    half = QK_ROPE // 2
    return jnp.concatenate([w[..., half:], w[..., :half]], axis=-1)


def _arrange_w_in(w):
    d = w.shape[0]
    w = w.astype(BF16)
    wqr = w[:, C_QR:C_CKV].reshape(d, B_HEADS, QK_ROPE)
    wkr = w[:, C_KR:C_END]
    return w, jnp.concatenate([wkr, _swap_halves(wkr), _swap_halves(wqr).reshape(d, B_QR)], axis=1)


def _arrange_w_kvb(w):
    w3 = w.reshape(KV_RANK, B_HEADS, QK_NOPE + V_DIM)
    w_k = w3[..., :QK_NOPE].reshape(KV_RANK, -1).astype(BF16)
    w_vt = w3[..., QK_NOPE:].reshape(KV_RANK, -1).T.astype(BF16)
    return w_k, w_vt


def _trunk(x, mod3, row_off, layers, bias2, rope, g_final):
    nbatch, seq, _ = x.shape
    x2d = x.reshape(nbatch * seq, D_MODEL)
    rope_a, rope_b = rope
    for l, p in enumerate(layers):
        kw = dict(row_off=row_off, seq=seq)
        qa, ka, va, qm, km, vt = _in_proj(x2d, mod3[l], p["g_mix"], p["g_kv"], *p["w_in"], *p["w_kv_b"],
                                          rope_a, rope_b, **kw)
        out_a = _win_attn(qa, ka, va, bias2, p["sink"], seq=seq, nbatch=nbatch)
        out_b = _mla_attn(qm, km, vt, seq=seq, nbatch=nbatch)
        x1, h2 = _out_proj(x2d, mod3[l], out_a, out_b, p["w_o"], p["g_mlp"], **kw)
        x2d = _mlp(h2, x1, mod3[l], p["w_ff1"], p["w_ff2"], g_final, final_norm=(l == len(layers) - 1), **kw)
    return x2d.reshape(nbatch, seq, D_MODEL)


def kernel(x_prompt, x_sample, c_prompt, c_sample, w_ada, b_ada, g_mix, w_in, sink, g_kv, w_kv_b, w_o, g_mlp,
           w_ff1, w_ff2, rel_bias, g_final):
    n_prompt, n_sample = c_prompt.shape[0], c_sample.shape[0]
    rows = -(-(n_prompt + n_sample) // 16) * 16
    c_all = jnp.concatenate(
        [c_prompt, c_sample, jnp.zeros((rows - n_prompt - n_sample, D_MODEL), F32)], axis=0)

    depth = w_in.shape[0]
    mod3, layers = [], []
    for l in range(depth):
        mod = _ada_mod(c_all, w_ada[l], b_ada[l].reshape(1, -1))
        mod3.append(mod.reshape(rows, 6, D_MODEL))
        sink_rows = jnp.repeat(sink[l].astype(F32), BLOCK).reshape(A_KV_HEADS, A_GROUP * BLOCK, 1)
        layers.append(dict(
            g_mix=g_mix[l].reshape(1, -1), g_kv=g_kv[l].reshape(1, -1), g_mlp=g_mlp[l].reshape(1, -1),
            w_in=_arrange_w_in(w_in[l]), w_kv_b=_arrange_w_kvb(w_kv_b[l]), w_o=w_o[l].astype(BF16),
            w_ff1=w_ff1[l].astype(BF16), w_ff2=w_ff2[l].astype(BF16),
            sink=jnp.broadcast_to(sink_rows, (A_KV_HEADS, A_GROUP * BLOCK, LANES)),
        ))
    bias2 = _bias_table(_bucket_table(), rel_bias.astype(F32)).reshape(A_KV_HEADS, A_GROUP * BLOCK, 3 * BLOCK)
    gfin = g_final.reshape(1, -1)

    rope = _rope_tables(max(x_prompt.shape[1], x_sample.shape[1]))

    y_prompt = _trunk(x_prompt, mod3, 0, layers, bias2, rope, gfin)
    y_sample = _trunk(x_sample, mod3, n_prompt, layers, bias2, rope, gfin)
    return (y_prompt, y_sample)
```
